```python
import math
import jax, jax.numpy as jnp
from jax import lax
import numpy as np

D_MODEL = 1024
BATCH = 2
SEQ = 8192
DEPTH = 2
DEC_BATCH = 32
DEC_SEQ = 1
PAST_LEN = 16384
PAGE_SIZE = 128

N_HEADS = 16
HEAD_DIM = D_MODEL // N_HEADS
D_FF = ((8 * D_MODEL // 3 + 127) // 128) * 128
CONV_W = 3
BLOCK_Q = 128
N_A = DEPTH // 2
N_B = DEPTH - N_A
EPS = 1e-6
SB_BIAS_HI = -2.0
SB_BIAS_LO = -9.0

kernel_name = "yoco_shortconv_stickbreaking_convffn_step"


def rms_norm(x, g):
    xf = x.astype(jnp.float32)
    y = xf * lax.rsqrt(jnp.mean(xf * xf, axis=-1, keepdims=True) + EPS)
    return (y * g.astype(jnp.float32)).astype(x.dtype)


def causal_dwconv(u, w, prev):
    T = u.shape[1]
    ext = jnp.concatenate([prev.astype(u.dtype), u], axis=1)
    y = sum(w[i] * ext[:, i:i + T] for i in range(CONV_W))
    return y, ext[:, -(CONV_W - 1):]


def short_conv_mixer(h, w_in, conv_w, w_out, prev):
    b, c, u = jnp.split(h @ w_in, 3, axis=-1)
    y, new_state = causal_dwconv(c * u, conv_w, prev)
    return (b * y) @ w_out, new_state


def conv_ffn(h, w_in, conv_w, w_down, prev):
    g, u = jnp.split(h @ w_in, 2, axis=-1)
    g_c, new_state = causal_dwconv(g, conv_w, prev)
    return (jax.nn.silu(g_c) * u) @ w_down, new_state


def shared_kv(x, kv_norm, w_kv, k_norm):
    B, T, _ = x.shape
    k, v = jnp.split(rms_norm(x, kv_norm) @ w_kv, 2, axis=-1)
    k = rms_norm(k.reshape(B, T, N_HEADS, HEAD_DIM), k_norm)
    v = v.reshape(B, T, N_HEADS, HEAD_DIM)
    return k, v


def sb_weights(z, mask):
    log_beta = jax.nn.log_sigmoid(z)
    log_keep = jnp.where(mask, log_beta - z, 0.0)
    later = lax.cumsum(log_keep, axis=z.ndim - 1, reverse=True) - log_keep
    return jnp.where(mask, jnp.exp(log_beta + later), 0.0)


def sb_prompt(q, k, v, bias):
    B, T, H, Dh = q.shape
    nb = T // BLOCK_Q
    qb = q.reshape(B, nb, BLOCK_Q, H, Dh).transpose(1, 0, 2, 3, 4)
    pos = jnp.arange(T, dtype=jnp.int32).reshape(nb, BLOCK_Q)
    k_pos = jnp.arange(T, dtype=jnp.int32)
    scale = HEAD_DIM ** -0.5
    b_h = bias.astype(jnp.float32)[None, :, None, None]

    def one_block(args):
        q_blk, q_pos = args
        z = jnp.einsum('bqhd,bkhd->bhqk', q_blk, k).astype(jnp.float32) * scale + b_h
        a = sb_weights(z, k_pos[None, :] < q_pos[:, None])
        return jnp.einsum('bhqk,bkhd->bqhd', a.astype(v.dtype), v)

    out = lax.map(one_block, (qb, pos))
    return out.transpose(1, 0, 2, 3, 4).reshape(B, T, H, Dh)


def trunk(x, conv_a_prev, ffn_prev, attend, a_norm, a_w_in, a_conv_w, a_w_out, kv_norm, w_kv,
          k_norm, b_norm, b_w_q, b_q_norm, b_sb_bias, b_w_o, ffn_norm, ffn_w_in, ffn_conv_w,
          ffn_w_down):
    B, T, _ = x.shape
    conv_a_new, ffn_new = [], []
    k = v = None
    for l in range(DEPTH):
        if l < N_A:
            h, s = short_conv_mixer(rms_norm(x, a_norm[l]), a_w_in[l], a_conv_w[l], a_w_out[l],
                                    conv_a_prev[l])
            conv_a_new.append(s)
        else:
            j = l - N_A
            q = (rms_norm(x, b_norm[j]) @ b_w_q[j]).reshape(B, T, N_HEADS, HEAD_DIM)
            q = rms_norm(q, b_q_norm[j])
            h = attend(q, k, v, b_sb_bias[j]).reshape(B, T, D_MODEL) @ b_w_o[j]
        x = x + h
        f, s = conv_ffn(rms_norm(x, ffn_norm[l]), ffn_w_in[l], ffn_conv_w[l], ffn_w_down[l],
                        ffn_prev[l])
        ffn_new.append(s)
        x = x + f
        if l == N_A - 1:
            k, v = shared_kv(x, kv_norm, w_kv, k_norm)
    return x, k, v, jnp.stack(conv_a_new), jnp.stack(ffn_new)


def setup_inputs(seed: int = 0) -> dict:
    key = jax.random.key(seed)
    ks = jax.random.split(key, 24)
    f32 = jnp.float32

    def nrm(k, shape, scale):
        return jax.random.normal(k, shape, f32) * scale

    def gain(k, shape):
        return 1.0 + 0.05 * jax.random.normal(k, shape, f32)

    n_pages = PAST_LEN // PAGE_SIZE
    n_used = DEC_BATCH * n_pages
    n_pool = n_used + (n_used + 3) // 4
    page_table = jax.random.permutation(ks[6], n_pool)[:n_used].reshape(DEC_BATCH, n_pages).astype(jnp.int32)
    sb_bias = (jnp.linspace(SB_BIAS_HI, SB_BIAS_LO, N_HEADS, dtype=f32)[None, :]
               + 0.1 * jax.random.normal(ks[22], (N_B, N_HEADS), f32))

    return {
        "x_prompt": nrm(ks[0], (BATCH, SEQ, D_MODEL), 1.0),
        "x_sample": nrm(ks[1], (DEC_BATCH, DEC_SEQ, D_MODEL), 1.0),
        "state_conv_a": nrm(ks[2], (N_A, DEC_BATCH, CONV_W - 1, D_MODEL), 1.0),
        "state_ffn_conv": nrm(ks[3], (DEPTH, DEC_BATCH, CONV_W - 1, D_FF), 1.0),
        "cache_k": nrm(ks[4], (n_pool, PAGE_SIZE, N_HEADS, HEAD_DIM), 1.0),
        "cache_v": nrm(ks[5], (n_pool, PAGE_SIZE, N_HEADS, HEAD_DIM), 1.0),
        "page_table": page_table,
        "a_norm": gain(ks[7], (N_A, D_MODEL)),
        "a_w_in": nrm(ks[8], (N_A, D_MODEL, 3 * D_MODEL), D_MODEL ** -0.5),
        "a_conv_w": nrm(ks[9], (N_A, CONV_W, D_MODEL), CONV_W ** -0.5),
        "a_w_out": nrm(ks[10], (N_A, D_MODEL, D_MODEL), D_MODEL ** -0.5),
        "kv_norm": gain(ks[11], (D_MODEL,)),
        "w_kv": nrm(ks[12], (D_MODEL, 2 * D_MODEL), D_MODEL ** -0.5),
        "k_norm": gain(ks[13], (HEAD_DIM,)),
        "b_norm": gain(ks[14], (N_B, D_MODEL)),
        "b_w_q": nrm(ks[15], (N_B, D_MODEL, D_MODEL), D_MODEL ** -0.5),
        "b_q_norm": gain(ks[16], (N_B, HEAD_DIM)),
        "b_sb_bias": sb_bias,
        "b_w_o": nrm(ks[17], (N_B, D_MODEL, D_MODEL), D_MODEL ** -0.5),
        "ffn_norm": gain(ks[18], (DEPTH, D_MODEL)),
        "ffn_w_in": nrm(ks[19], (DEPTH, D_MODEL, 2 * D_FF), D_MODEL ** -0.5),
        "ffn_conv_w": nrm(ks[20], (DEPTH, CONV_W, D_FF), CONV_W ** -0.5),
        "ffn_w_down": nrm(ks[21], (DEPTH, D_FF, D_MODEL), D_FF ** -0.5),
    }


def reference(x_prompt, x_sample, state_conv_a, state_ffn_conv, cache_k, cache_v, page_table,
              a_norm, a_w_in, a_conv_w, a_w_out, kv_norm, w_kv, k_norm, b_norm, b_w_q, b_q_norm,
              b_sb_bias, b_w_o, ffn_norm, ffn_w_in, ffn_conv_w, ffn_w_down):
    scale = HEAD_DIM ** -0.5

    def sample_attend(q, k_new, v_new, bias):
        db, n_pages = page_table.shape
        past = n_pages * cache_k.shape[1]
        kp = cache_k[page_table].reshape(db, past, N_HEADS, HEAD_DIM)
        vp = cache_v[page_table].reshape(db, past, N_HEADS, HEAD_DIM)
        tq = q.shape[1]
        z = jnp.concatenate([
            jnp.einsum('bqhd,bkhd->bhqk', q, kp),
            jnp.einsum('bqhd,bkhd->bhqk', q, k_new)], axis=-1).astype(jnp.float32) * scale
        z = z + bias.astype(jnp.float32)[None, :, None, None]
        q_pos = past + jnp.arange(tq, dtype=jnp.int32)
        k_pos = jnp.arange(past + tq, dtype=jnp.int32)
        a = sb_weights(z, k_pos[None, :] < q_pos[:, None]).astype(v_new.dtype)
        return (jnp.einsum('bhqk,bkhd->bqhd', a[..., :past], vp)
                + jnp.einsum('bhqk,bkhd->bqhd', a[..., past:], v_new))

    bp = x_prompt.shape[0]
    zeros_a = jnp.zeros((N_A, bp, CONV_W - 1, D_MODEL), x_prompt.dtype)
    zeros_f = jnp.zeros((DEPTH, bp, CONV_W - 1, D_FF), x_prompt.dtype)
    weights = (a_norm, a_w_in, a_conv_w, a_w_out, kv_norm, w_kv, k_norm, b_norm, b_w_q, b_q_norm,
               b_sb_bias, b_w_o, ffn_norm, ffn_w_in, ffn_conv_w, ffn_w_down)

    y_prompt, k_prompt, v_prompt, conv_a_prompt, ffn_conv_prompt = trunk(
        x_prompt, zeros_a, zeros_f, sb_prompt, *weights)
    y_sample, k_sample, v_sample, conv_a_sample, ffn_conv_sample = trunk(
        x_sample, state_conv_a, state_ffn_conv, sample_attend, *weights)

    return (y_prompt, y_sample, k_prompt, v_prompt, k_sample, v_sample,
            conv_a_prompt, conv_a_sample, ffn_conv_prompt, ffn_conv_sample)
```

```python
import functools

import jax
import jax.numpy as jnp
from jax import lax
from jax.experimental import pallas as pl
from jax.experimental.pallas import tpu as pltpu

EPS = 1e-6
CONV_TAPS = 3
LOG2E = 1.4426950408889634

V7X_LANES = 128
V7X_SUBLANES = 8
V7X_MXU_DIM = 256
V7X_VMEM_BYTES = 64 * 1024 * 1024

ROW_TILE = 512
COL_CHUNK = V7X_MXU_DIM
ATT_TQ = V7X_MXU_DIM
ATT_TK = V7X_MXU_DIM
ATT_HEADS = 4
DEC_PAGES = 8
VMEM_LIMIT = 56 * 1024 * 1024

BF16 = jnp.bfloat16
F32 = jnp.float32


def _params(*sem):
    return pltpu.CompilerParams(dimension_semantics=sem, vmem_limit_bytes=VMEM_LIMIT)


def _full(shape):
    nd = len(shape)
    return pl.BlockSpec(shape, lambda *_: (0,) * nd, pipeline_mode=pl.Buffered(1))


def _dot(a, b):
    return jnp.dot(a, b, preferred_element_type=F32)


def _dot_nt(a, b):
    return lax.dot_general(a, b, (((1,), (1,)), ((), ())), preferred_element_type=F32)


def _rms(x, gain):
    ms = jnp.mean(x * x, axis=-1, keepdims=True)
    return x * lax.rsqrt(ms + EPS) * gain


def _softplus2(z):
    return jnp.maximum(z, 0.0) + jnp.log2(1.0 + jnp.exp2(-jnp.abs(z)))


def _silu(x):
    return x * jax.nn.sigmoid(x)


def _conv_seq(u, w, prev):
    row = lax.broadcasted_iota(jnp.int32, u.shape, 0)
    p0 = prev[6:7, :]
    p1 = prev[7:8, :]
    r1 = jnp.where(row == 0, p1, pltpu.roll(u, 1, axis=0))
    r2 = jnp.where(row == 0, p0, jnp.where(row == 1, p1, pltpu.roll(u, 2, axis=0)))
    return w[0:1, :] * r2 + w[1:2, :] * r1 + w[2:3, :] * u


def _conv_step(u, w, p0, p1):
    return w[0:1, :] * p0 + w[1:2, :] * p1 + w[2:3, :] * u


def _mixer_chunks(hn, d, win_ref, wout_ref, conv_fn):
    acc = None
    for j in range(d // COL_CHUNK):
        lo = j * COL_CHUNK
        b = _dot(hn, win_ref[:, lo:lo + COL_CHUNK])
        c = _dot(hn, win_ref[:, d + lo:d + lo + COL_CHUNK])
        u = _dot(hn, win_ref[:, 2 * d + lo:2 * d + lo + COL_CHUNK])
        cu = c * u
        y = conv_fn(cu, lo)
        part = _dot((b * y).astype(BF16), wout_ref[lo:lo + COL_CHUNK, :])
        acc = part if acc is None else acc + part
    return acc


def _mixer_seq_kernel(x_ref, g_ref, win_ref, cw_ref, wout_ref, o_ref, st_ref, prev_sc):
    tm, d = x_ref.shape[1], x_ref.shape[2]

    @pl.when(pl.program_id(1) == 0)
    def _():
        prev_sc[...] = jnp.zeros_like(prev_sc)

    x = x_ref[0]
    hn = _rms(x, g_ref[...]).astype(BF16)

    def conv_fn(cu, lo):
        sl = slice(lo, lo + COL_CHUNK)
        y = _conv_seq(cu, cw_ref[:, sl], prev_sc[:, sl])
        tail = cu[tm - V7X_SUBLANES:, :]
        prev_sc[:, sl] = tail
        st_ref[0, :, sl] = tail
        return y

    o_ref[0] = x + _mixer_chunks(hn, d, win_ref, wout_ref, conv_fn)


def _mixer_seq(x, gain, w_in, conv_w, w_out):
    b, t, d = x.shape
    tm = min(ROW_TILE, t)
    return pl.pallas_call(
        _mixer_seq_kernel,
        grid=(b, t // tm),
        in_specs=[pl.BlockSpec((1, tm, d), lambda i, j: (i, j, 0)),
                  _full((1, d)), _full((d, 3 * d)), _full((CONV_TAPS, d)), _full((d, d))],
        out_specs=[pl.BlockSpec((1, tm, d), lambda i, j: (i, j, 0)),
                   pl.BlockSpec((1, V7X_SUBLANES, d), lambda i, j: (i, 0, 0))],
        out_shape=[jax.ShapeDtypeStruct((b, t, d), F32),
                   jax.ShapeDtypeStruct((b, V7X_SUBLANES, d), F32)],
        scratch_shapes=[pltpu.VMEM((V7X_SUBLANES, d), F32)],
        compiler_params=_params("arbitrary", "arbitrary"),
        name="mixer_seq",
    )(x, gain, w_in, conv_w, w_out)


def _mixer_step_kernel(x_ref, p0_ref, p1_ref, g_ref, win_ref, cw_ref, wout_ref, o_ref, cu_ref):
    d = x_ref.shape[1]
    x = x_ref[...]
    hn = _rms(x, g_ref[...]).astype(BF16)

    def conv_fn(cu, lo):
        sl = slice(lo, lo + COL_CHUNK)
        cu_ref[:, sl] = cu
        return _conv_step(cu, cw_ref[:, sl], p0_ref[:, sl], p1_ref[:, sl])

    o_ref[...] = x + _mixer_chunks(hn, d, win_ref, wout_ref, conv_fn)


def _mixer_step(x, p0, p1, gain, w_in, conv_w, w_out):
    m, d = x.shape
    return pl.pallas_call(
        _mixer_step_kernel,
        out_shape=[jax.ShapeDtypeStruct((m, d), F32), jax.ShapeDtypeStruct((m, d), F32)],
        compiler_params=pltpu.CompilerParams(vmem_limit_bytes=VMEM_LIMIT),
        name="mixer_step",
    )(x, p0, p1, gain, w_in, conv_w, w_out)


def _ffn_chunks(hn, f, win_ref, wdown_ref, conv_fn):
    acc = None
    for j in range(f // COL_CHUNK):
        lo = j * COL_CHUNK
        g = _dot(hn, win_ref[:, lo:lo + COL_CHUNK])
        u = _dot(hn, win_ref[:, f + lo:f + lo + COL_CHUNK])
        gc = conv_fn(g, lo)
        part = _dot((_silu(gc) * u).astype(BF16), wdown_ref[lo:lo + COL_CHUNK, :])
        acc = part if acc is None else acc + part
    return acc


def _ffn_seq_body(x, g_ref, win_ref, cw_ref, wdown_ref, o_ref, st_ref, prev_sc):
    tm = x.shape[0]
    f = wdown_ref.shape[0]

    @pl.when(pl.program_id(1) == 0)
    def _():
        prev_sc[...] = jnp.zeros_like(prev_sc)

    hn = _rms(x, g_ref[...]).astype(BF16)

    def conv_fn(g, lo):
        sl = slice(lo, lo + COL_CHUNK)
        y = _conv_seq(g, cw_ref[:, sl], prev_sc[:, sl])
        tail = g[tm - V7X_SUBLANES:, :]
        prev_sc[:, sl] = tail
        st_ref[0, :, sl] = tail
        return y

    o_ref[0] = x + _ffn_chunks(hn, f, win_ref, wdown_ref, conv_fn)


def _ffn_seq_kernel(x_ref, g_ref, win_ref, cw_ref, wdown_ref, o_ref, st_ref, prev_sc):
    _ffn_seq_body(x_ref[0], g_ref, win_ref, cw_ref, wdown_ref, o_ref, st_ref, prev_sc)


def _attn_ffn_seq_kernel(x_ref, a_ref, wo_ref, g_ref, win_ref, cw_ref, wdown_ref, o_ref, st_ref,
                         prev_sc):
    x = x_ref[0] + _dot(a_ref[0], wo_ref[...])
    _ffn_seq_body(x, g_ref, win_ref, cw_ref, wdown_ref, o_ref, st_ref, prev_sc)


def _ffn_seq(x, attn, w_o, gain, w_in, conv_w, w_down):
    b, t, d = x.shape
    f = w_down.shape[0]
    tm = min(ROW_TILE, t)
    row = pl.BlockSpec((1, tm, d), lambda i, j: (i, j, 0))
    weights = [_full((1, d)), _full((d, 2 * f)), _full((CONV_TAPS, f)), _full((f, d))]
    if attn is None:
        body, ins, specs = _ffn_seq_kernel, (x,), [row]
    else:
        body, ins, specs = _attn_ffn_seq_kernel, (x, attn, w_o), [row, row, _full((d, d))]
    return pl.pallas_call(
        body,
        grid=(b, t // tm),
        in_specs=specs + weights,
        out_specs=[row, pl.BlockSpec((1, V7X_SUBLANES, f), lambda i, j: (i, 0, 0))],
        out_shape=[jax.ShapeDtypeStruct((b, t, d), F32),
                   jax.ShapeDtypeStruct((b, V7X_SUBLANES, f), F32)],
        scratch_shapes=[pltpu.VMEM((V7X_SUBLANES, f), F32)],
        compiler_params=_params("arbitrary", "arbitrary"),
        name="ffn_seq",
    )(*ins, gain, w_in, conv_w, w_down)


def _ffn_step_body(x, p0_ref, p1_ref, g_ref, win_ref, cw_ref, wdown_ref, o_ref, gate_ref):
    f = wdown_ref.shape[0]
    hn = _rms(x, g_ref[...]).astype(BF16)

    def conv_fn(g, lo):
        sl = slice(lo, lo + COL_CHUNK)
        gate_ref[:, sl] = g
        return _conv_step(g, cw_ref[:, sl], p0_ref[:, sl], p1_ref[:, sl])

    o_ref[...] = x + _ffn_chunks(hn, f, win_ref, wdown_ref, conv_fn)


def _ffn_step_kernel(x_ref, p0_ref, p1_ref, g_ref, win_ref, cw_ref, wdown_ref, o_ref, gate_ref):
    _ffn_step_body(x_ref[...], p0_ref, p1_ref, g_ref, win_ref, cw_ref, wdown_ref, o_ref, gate_ref)


def _attn_ffn_step_kernel(x_ref, a_ref, wo_ref, p0_ref, p1_ref, g_ref, win_ref, cw_ref, wdown_ref,
                          o_ref, gate_ref):
    x = x_ref[...] + _dot(a_ref[...].astype(BF16), wo_ref[...])
    _ffn_step_body(x, p0_ref, p1_ref, g_ref, win_ref, cw_ref, wdown_ref, o_ref, gate_ref)


def _ffn_step(x, attn, w_o, p0, p1, gain, w_in, conv_w, w_down):
    m, d = x.shape
    f = w_down.shape[0]
    if attn is None:
        body, ins = _ffn_step_kernel, (x,)
    else:
        body, ins = _attn_ffn_step_kernel, (x, attn, w_o)
    return pl.pallas_call(
        body,
        out_shape=[jax.ShapeDtypeStruct((m, d), F32), jax.ShapeDtypeStruct((m, f), F32)],
        compiler_params=pltpu.CompilerParams(vmem_limit_bytes=VMEM_LIMIT),
        name="ffn_step",
    )(*ins, p0, p1, gain, w_in, conv_w, w_down)


def _head_norm(t, avg_ref, gain):
    ms = _dot((t * t).astype(BF16), avg_ref[...])
    return t * lax.rsqrt(ms + EPS) * gain


def _kv_seq_kernel(x_ref, n_ref, w_ref, avg_ref, kg_ref, kt_ref, vt_ref, kb_ref, vtb_ref):
    tm, d = x_ref.shape[1], x_ref.shape[2]
    hn = _rms(x_ref[0], n_ref[...]).astype(BF16)
    k = _head_norm(_dot(hn, w_ref[:, :d]), avg_ref, kg_ref[...])
    v = _dot(hn, w_ref[:, d:])
    kb_ref[0] = k.astype(BF16)
    kt_ref[0] = k.T
    vt = v.T
    vt_ref[0] = vt
    for j in range(tm // ATT_TK):
        vtb_ref[0, j] = vt[:, j * ATT_TK:(j + 1) * ATT_TK].astype(BF16)


def _kv_seq(x, gain, w_kv, avg, k_gain):
    b, t, d = x.shape
    tm = min(ROW_TILE, t)
    nkb = tm // ATT_TK
    return pl.pallas_call(
        _kv_seq_kernel,
        grid=(b, t // tm),
        in_specs=[pl.BlockSpec((1, tm, d), lambda i, j: (i, j, 0)),
                  _full((1, d)), _full((d, 2 * d)), _full((d, d)), _full((1, d))],
        out_specs=[pl.BlockSpec((1, d, tm), lambda i, j: (i, 0, j)),
                   pl.BlockSpec((1, d, tm), lambda i, j: (i, 0, j)),
                   pl.BlockSpec((1, tm, d), lambda i, j: (i, j, 0)),
                   pl.BlockSpec((1, nkb, d, ATT_TK), lambda i, j: (i, j, 0, 0))],
        out_shape=[jax.ShapeDtypeStruct((b, d, t), F32),
                   jax.ShapeDtypeStruct((b, d, t), F32),
                   jax.ShapeDtypeStruct((b, t, d), BF16),
                   jax.ShapeDtypeStruct((b, t // ATT_TK, d, ATT_TK), BF16)],
        compiler_params=_params("arbitrary", "arbitrary"),
        name="kv_seq",
    )(x, gain, w_kv, avg, k_gain)


def _q_seq_kernel(scale2, x_ref, n_ref, w_ref, avg_ref, qg_ref, qt_ref):
    hn = _rms(x_ref[0], n_ref[...]).astype(BF16)
    q = _head_norm(_dot(hn, w_ref[...]), avg_ref, qg_ref[...]) * scale2
    qt_ref[0] = q.T.astype(BF16)


def _q_seq(x, gain, w_q, avg, q_gain, scale2):
    b, t, d = x.shape
    tm = min(ROW_TILE, t)
    return pl.pallas_call(
        functools.partial(_q_seq_kernel, scale2),
        grid=(b, t // tm),
        in_specs=[pl.BlockSpec((1, tm, d), lambda i, j: (i, j, 0)),
                  _full((1, d)), _full((d, d)), _full((d, d)), _full((1, d))],
        out_specs=pl.BlockSpec((1, d, tm), lambda i, j: (i, 0, j)),
        out_shape=jax.ShapeDtypeStruct((b, d, t), BF16),
        compiler_params=_params("arbitrary", "arbitrary"),
        name="q_seq",
    )(x, gain, w_q, avg, q_gain)


def _kv_step_kernel(x_ref, n_ref, w_ref, avg_ref, kg_ref, k_ref, v_ref):
    d = x_ref.shape[1]
    hn = _rms(x_ref[...], n_ref[...]).astype(BF16)
    k_ref[...] = _head_norm(_dot(hn, w_ref[:, :d]), avg_ref, kg_ref[...])
    v_ref[...] = _dot(hn, w_ref[:, d:])


def _kv_step(x, gain, w_kv, avg, k_gain):
    m, d = x.shape
    return pl.pallas_call(
        _kv_step_kernel,
        out_shape=[jax.ShapeDtypeStruct((m, d), F32), jax.ShapeDtypeStruct((m, d), F32)],
        compiler_params=pltpu.CompilerParams(vmem_limit_bytes=VMEM_LIMIT),
        name="kv_step",
    )(x, gain, w_kv, avg, k_gain)


def _q_step_kernel(scale2, x_ref, n_ref, w_ref, avg_ref, qg_ref, q_ref):
    hn = _rms(x_ref[...], n_ref[...]).astype(BF16)
    q_ref[...] = _head_norm(_dot(hn, w_ref[...]), avg_ref, qg_ref[...]) * scale2


def _q_step(x, gain, w_q, avg, q_gain, scale2):
    m, d = x.shape
    return pl.pallas_call(
        functools.partial(_q_step_kernel, scale2),
        out_shape=jax.ShapeDtypeStruct((m, d), F32),
        compiler_params=pltpu.CompilerParams(vmem_limit_bytes=VMEM_LIMIT),
        name="q_step",
    )(x, gain, w_q, avg, q_gain)


def _attn_seq_kernel(dh, bias_ref, qt_ref, k_ref, vt_ref, tri_ref, o_ref, c_sc, acc_sc):
    g = pl.program_id(1)
    qi = pl.program_id(2)
    tq = qt_ref.shape[2]
    width = qt_ref.shape[1]
    nh = width // dh

    c_sc[...] = jnp.zeros_like(c_sc)
    acc_sc[...] = jnp.zeros_like(acc_sc)

    qt = qt_ref[0]
    head_of_row = lax.broadcasted_iota(jnp.int32, (width, tq), 0) // dh
    q_heads = [jnp.where(head_of_row == e, qt, jnp.zeros_like(qt)) for e in range(nh)]
    bias2 = [bias_ref[g * nh + e] * LOG2E for e in range(nh)]
    tri = tri_ref[...]

    def block(kb, diagonal):
        kblk = k_ref[0, pl.ds(pl.multiple_of(kb * ATT_TK, ATT_TK), ATT_TK), :]
        if diagonal:
            key = lax.broadcasted_iota(jnp.int32, (ATT_TK, tq), 0)
            qry = lax.broadcasted_iota(jnp.int32, (ATT_TK, tq), 1)
            valid = key < qry
        for e in range(nh):
            z = _dot(kblk, q_heads[e]) + bias2[e]
            sp = _softplus2(z)
            if diagonal:
                sp = jnp.where(valid, sp, 0.0)
            s_local = _dot(tri, sp.astype(BF16))
            carry = c_sc[e:e + 1, :]
            a = jnp.exp2(z - (s_local + carry))
            if diagonal:
                a = jnp.where(valid, a, 0.0)
            c_sc[e:e + 1, :] = carry + s_local[0:1, :]
            rows = slice(e * dh, (e + 1) * dh)
            acc_sc[rows, :] += _dot(vt_ref[0, kb, rows, :], a.astype(BF16))

    block(qi, True)

    def body(i, _):
        block(qi - 1 - i, False)
        return 0

    lax.fori_loop(0, qi, body, 0)
    o_ref[0] = acc_sc[...].T.astype(BF16)


def _attn_seq(qt, kb, vtb, bias, dh):
    b, d, t = qt.shape
    width = ATT_HEADS * dh
    nkb = t // ATT_TK
    row = lax.broadcasted_iota(jnp.int32, (ATT_TK, ATT_TK), 0)
    col = lax.broadcasted_iota(jnp.int32, (ATT_TK, ATT_TK), 1)
    tri = (col >= row).astype(BF16)
    grid_spec = pltpu.PrefetchScalarGridSpec(
        num_scalar_prefetch=1,
        grid=(b, d // width, t // ATT_TQ),
        in_specs=[pl.BlockSpec((1, width, ATT_TQ), lambda i, g, q, *_: (i, g, q)),
                  pl.BlockSpec((1, t, width), lambda i, g, q, *_: (i, 0, g)),
                  pl.BlockSpec((1, nkb, width, ATT_TK), lambda i, g, q, *_: (i, 0, g, 0)),
                  pl.BlockSpec((ATT_TK, ATT_TK), lambda i, g, q, *_: (0, 0))],
        out_specs=pl.BlockSpec((1, ATT_TQ, width), lambda i, g, q, *_: (i, q, g)),
        scratch_shapes=[pltpu.VMEM((V7X_SUBLANES, ATT_TQ), F32),
                        pltpu.VMEM((width, ATT_TQ), F32)],
    )
    return pl.pallas_call(
        functools.partial(_attn_seq_kernel, dh),
        grid_spec=grid_spec,
        out_shape=jax.ShapeDtypeStruct((b, t, d), BF16),
        compiler_params=_params("arbitrary", "arbitrary", "arbitrary"),
        name="attn_seq",
    )(bias, qt, kb, vtb, tri)


def _rev_cumsum_lanes(x):
    n = x.shape[1]
    lane = lax.broadcasted_iota(jnp.int32, x.shape, 1)
    s = 1
    while s < n:
        shifted = pltpu.roll(x, n - s, axis=1)
        x = x + jnp.where(lane < n - s, shifted, 0.0)
        s *= 2
    return x


def _decode_kernel(npages, past, pt_ref, q_ref, bias_ref, knew_ref, vnew_ref, *refs):
    k_refs = refs[:npages]
    v_refs = refs[npages:2 * npages]
    o_ref = refs[2 * npages]
    qbd_sc, a_sc, c_sc, anew_sc, acc_sc = refs[2 * npages + 1:]
    step = pl.program_id(1)
    nh, dh, page = k_refs[0].shape[1:]
    d = nh * dh

    @pl.when(step == 0)
    def _():
        head = lax.broadcasted_iota(jnp.int32, (nh, d), 0)
        col_head = lax.broadcasted_iota(jnp.int32, (nh, d), 1) // dh
        qbd = jnp.where(head == col_head, q_ref[0], 0.0).astype(BF16)
        qbd_sc[...] = qbd
        k_new = jnp.broadcast_to(knew_ref[0], (page, d)).astype(BF16)
        z_new = _dot_nt(qbd, k_new) + bias_ref[...]
        k_pos = past + lax.broadcasted_iota(jnp.int32, (nh, page), 0) * 0
        valid = k_pos < past
        sp_new = jnp.where(valid, _softplus2(z_new), 0.0)
        c_sc[...] = sp_new
        anew_sc[...] = jnp.where(valid, jnp.exp2(z_new - sp_new), 0.0)
        acc_sc[...] = jnp.zeros_like(acc_sc)

    qbd = qbd_sc[...]
    bias = bias_ref[...]
    zs = [_dot(qbd, k_refs[j][0].reshape(d, page).astype(BF16)) + bias for j in range(npages)]
    z = jnp.concatenate(zs, axis=1)
    s_local = _rev_cumsum_lanes(_softplus2(z))
    carry = c_sc[...]
    a_sc[...] = jnp.exp2(z - (s_local + carry[:, 0:1]))
    c_sc[...] = carry + s_local[:, 0:1]

    for h in range(nh):
        acc = acc_sc[h]
        for j in range(npages):
            acc = acc + v_refs[j][0, h] * a_sc[h:h + 1, j * page:(j + 1) * page]
        acc_sc[h] = acc

    @pl.when(step == pl.num_programs(1) - 1)
    def _():
        for h in range(nh):
            out = jnp.sum(acc_sc[h].T, axis=0, keepdims=True)
            o_ref[0, h:h + 1, :] = out + anew_sc[h:h + 1, 0:dh] * vnew_ref[0, h:h + 1, :]


def _decode_attn(q, bias2, k_new, v_new, cache_kt, cache_vt, page_table):
    m, d = q.shape
    _, nh, dh, page = cache_kt.shape
    n_pages = page_table.shape[1]
    npg = DEC_PAGES
    n_steps = n_pages // npg
    past = n_pages * page

    def page_spec(j):
        def index(i, s, pt):
            return (pt[i, n_pages - (s + 1) * npg + j], 0, 0, 0)
        return pl.BlockSpec((1, nh, dh, page), index)

    row = pl.BlockSpec((1, 1, d), lambda i, s, pt: (i, 0, 0))
    grid_spec = pltpu.PrefetchScalarGridSpec(
        num_scalar_prefetch=1,
        grid=(m, n_steps),
        in_specs=[row,
                  pl.BlockSpec((nh, page), lambda i, s, pt: (0, 0)),
                  row,
                  pl.BlockSpec((1, nh, dh), lambda i, s, pt: (i, 0, 0))]
                 + [page_spec(j) for j in range(npg)] * 2,
        out_specs=pl.BlockSpec((1, nh, dh), lambda i, s, pt: (i, 0, 0)),
        scratch_shapes=[pltpu.VMEM((nh, d), BF16),
                        pltpu.VMEM((nh, npg * page), F32),
                        pltpu.VMEM((nh, page), F32),
                        pltpu.VMEM((nh, page), F32),
                        pltpu.VMEM((nh, dh, page), F32)],
    )
    out = pl.pallas_call(
        functools.partial(_decode_kernel, npg, past),
        grid_spec=grid_spec,
        out_shape=jax.ShapeDtypeStruct((m, nh, dh), F32),
        compiler_params=_params("arbitrary", "arbitrary"),
        name="decode_attn",
    )(page_table, q.reshape(m, 1, d), bias2, k_new.reshape(m, 1, d), v_new.reshape(m, nh, dh),
      *([cache_kt] * npg), *([cache_vt] * npg))
    return out.reshape(m, d)


def kernel(x_prompt, x_sample, state_conv_a, state_ffn_conv, cache_k, cache_v, page_table, a_norm, a_w_in, a_conv_w, a_w_out, kv_norm, w_kv, k_norm, b_norm, b_w_q, b_q_norm, b_sb_bias, b_w_o, ffn_norm, ffn_w_in, ffn_conv_w, ffn_w_down):
    b, t, d = x_prompt.shape
    m = x_sample.shape[0]
    assert x_sample.shape[1] == 1, "sample path handles one new token per sequence"
    _, page, nh, dh = cache_k.shape
    depth = ffn_norm.shape[0]
    n_a = a_norm.shape[0]
    assert d == nh * dh and (ATT_HEADS * dh) % V7X_MXU_DIM == 0
    assert t % ROW_TILE == 0 and t % ATT_TQ == 0 and ATT_TQ == ATT_TK
    assert page == V7X_LANES and page_table.shape[1] % DEC_PAGES == 0
    scale2 = dh ** -0.5 * LOG2E

    cache_kt = jnp.transpose(cache_k, (0, 2, 3, 1))
    cache_vt = jnp.transpose(cache_v, (0, 2, 3, 1))

    head_id = jnp.arange(d, dtype=jnp.int32) // dh
    avg = jnp.where(head_id[:, None] == head_id[None, :], 1.0 / dh, 0.0).astype(BF16)
    k_gain = jnp.tile(k_norm, nh).reshape(1, d)
    bias2_dec = jnp.broadcast_to((b_sb_bias * LOG2E)[:, :, None], b_sb_bias.shape + (page,))

    a_w_in_b = a_w_in.astype(BF16)
    a_w_out_b = a_w_out.astype(BF16)
    w_kv_b = w_kv.astype(BF16)
    b_w_q_b = b_w_q.astype(BF16)
    b_w_o_b = b_w_o.astype(BF16)
    ffn_w_in_b = ffn_w_in.astype(BF16)
    ffn_w_down_b = ffn_w_down.astype(BF16)

    xp = x_prompt
    xs = x_sample.reshape(m, d)
    conv_a_p, conv_a_s, ffn_p, ffn_s = [], [], [], []
    kv_p = kv_s = None
    for l in range(depth):
        attn_p = attn_s = w_o = None
        if l < n_a:
            gain = a_norm[l].reshape(1, d)
            xp, st = _mixer_seq(xp, gain, a_w_in_b[l], a_conv_w[l], a_w_out_b[l])
            conv_a_p.append(st[:, V7X_SUBLANES - (CONV_TAPS - 1):, :])
            p0, p1 = state_conv_a[l, :, 0, :], state_conv_a[l, :, 1, :]
            xs, cu = _mixer_step(xs, p0, p1, gain, a_w_in_b[l], a_conv_w[l], a_w_out_b[l])
            conv_a_s.append(jnp.stack([p1, cu], axis=1))
        else:
            j = l - n_a
            gain = b_norm[j].reshape(1, d)
            q_gain = jnp.tile(b_q_norm[j], nh).reshape(1, d)
            kt, vt, kb, vtb = kv_p
            qt = _q_seq(xp, gain, b_w_q_b[j], avg, q_gain, scale2)
            attn_p = _attn_seq(qt, kb, vtb, b_sb_bias[j], dh)
            k_s, v_s = kv_s
            q_s = _q_step(xs, gain, b_w_q_b[j], avg, q_gain, scale2)
            attn_s = _decode_attn(q_s, bias2_dec[j], k_s, v_s, cache_kt, cache_vt, page_table)
            w_o = b_w_o_b[j]
        gain = ffn_norm[l].reshape(1, d)
        xp, st = _ffn_seq(xp, attn_p, w_o, gain, ffn_w_in_b[l], ffn_conv_w[l], ffn_w_down_b[l])
        ffn_p.append(st[:, V7X_SUBLANES - (CONV_TAPS - 1):, :])
        p0, p1 = state_ffn_conv[l, :, 0, :], state_ffn_conv[l, :, 1, :]
        xs, gate = _ffn_step(xs, attn_s, w_o, p0, p1, gain, ffn_w_in_b[l], ffn_conv_w[l],
                             ffn_w_down_b[l])
        ffn_s.append(jnp.stack([p1, gate], axis=1))
        if l == n_a - 1:
            kv_gain = kv_norm.reshape(1, d)
            kv_p = _kv_seq(xp, kv_gain, w_kv_b, avg, k_gain)
            kv_s = _kv_step(xs, kv_gain, w_kv_b, avg, k_gain)

    kt, vt = kv_p[0], kv_p[1]
    k_prompt = jnp.transpose(kt.reshape(b, nh, dh, t), (0, 3, 1, 2))
    v_prompt = jnp.transpose(vt.reshape(b, nh, dh, t), (0, 3, 1, 2))
    k_sample = kv_s[0].reshape(m, 1, nh, dh)
    v_sample = kv_s[1].reshape(m, 1, nh, dh)
    return (xp, xs.reshape(m, 1, d), k_prompt, v_prompt, k_sample, v_sample,
            jnp.stack(conv_a_p), jnp.stack(conv_a_s), jnp.stack(ffn_p), jnp.stack(ffn_s))
```
